```python
import jax, jax.numpy as jnp
from jax import lax
import numpy as np

D_MODEL = 2048
BATCH = 32
SEQ = 256
DEPTH = 2
DEC_BATCH = 2
DEC_SEQ = 1024
PAST_LEN = 256

GRID_W = 64
Q_BLOCK = 128
H_A = 8
Q_LORA = 512
KV_LORA = 512
NOPE_A = 128
ROPE_A = 64
V_A = 128
H_B = 8
KVH_B = 2
HD_B = 128
H_C = 8
HD_C = 128
NA_ROWS = 8
NA_COLS = 16
P_HEADS = 8
N_KEYS = 128
N_EXPERTS = N_KEYS * N_KEYS
D_KEY = 256
P_TOPK = 16
PEER_CHUNK = 128

ROPE_BASE = 10000.0
EPS = 1e-6
ALPHA = (2 * DEPTH) ** 0.25
BETA = (8 * DEPTH) ** -0.25
IN_WIDTHS = (Q_LORA, KV_LORA, ROPE_A,
             H_B * HD_B, KVH_B * HD_B, KVH_B * HD_B,
             H_C * HD_C, H_C * HD_C, H_C * HD_C,
             D_MODEL, D_MODEL, D_MODEL)
N_IN = sum(IN_WIDTHS)

kernel_name = "hybrid_mla_gqa_natten_peer_diffusion_step"


def rms_norm(x, g):
    xf = x.astype(jnp.float32)
    y = xf * lax.rsqrt(jnp.mean(xf * xf, axis=-1, keepdims=True) + EPS)
    return (y * g).astype(x.dtype)


def layer_norm(x, g, b):
    xf = x.astype(jnp.float32)
    mu = jnp.mean(xf, axis=-1, keepdims=True)
    var = jnp.mean(jnp.square(xf - mu), axis=-1, keepdims=True)
    return ((xf - mu) * lax.rsqrt(var + EPS) * g + b).astype(x.dtype)


def axial_rope(x):
    T = x.shape[1]
    t = jnp.arange(T)
    row = (t // GRID_W).astype(jnp.float32)
    col = (t % GRID_W).astype(jnp.float32)
    half = x.shape[-1] // 2
    nf = half // 2
    inv = jnp.power(ROPE_BASE, -jnp.arange(nf, dtype=jnp.float32) / nf)

    def rot(xh, pos):
        ang = pos[:, None] * inv[None, :]
        cos = jnp.cos(ang)[None, :, None, :].astype(x.dtype)
        sin = jnp.sin(ang)[None, :, None, :].astype(x.dtype)
        a, b = xh[..., :nf], xh[..., nf:]
        return jnp.concatenate([a * cos - b * sin, a * sin + b * cos], axis=-1)

    return jnp.concatenate([rot(x[..., :half], row), rot(x[..., half:], col)], axis=-1)


def block_attention(q, k, v):
    B, T, H, dq = q.shape
    G = k.shape[2]
    R = H // G
    dv = v.shape[-1]
    nb = T // Q_BLOCK
    scale = dq ** -0.5
    qb = jnp.moveaxis(q.reshape(B, nb, Q_BLOCK, G, R, dq), 1, 0)

    def one(qblk):
        s = jnp.einsum('bqgrd,bsgd->bgrqs', qblk, k).astype(jnp.float32) * scale
        p = jax.nn.softmax(s, axis=-1).astype(v.dtype)
        return jnp.einsum('bgrqs,bsgd->bqgrd', p, v)

    o = lax.map(one, qb)
    return jnp.moveaxis(o, 0, 1).reshape(B, T, H, dv)


def neighbourhood_attention(q, k, v, k_ctx, v_ctx, rpb):
    B, T, H, dh = q.shape
    rows = T // GRID_W
    wr = min(NA_ROWS, rows)
    scale = dh ** -0.5
    qg = q.reshape(B, rows, GRID_W, H, dh)
    kg = k.reshape(B, rows, GRID_W, H, dh)
    vg = v.reshape(B, rows, GRID_W, H, dh)
    col = jnp.arange(GRID_W)
    c0 = jnp.clip(col - NA_COLS // 2, 0, GRID_W - NA_COLS)
    col_idx = c0[:, None] + jnp.arange(NA_COLS)[None, :]
    dc_idx = col_idx - col[:, None] + (NA_COLS - 1)

    def one_row(args):
        r, q_r = args
        r0 = jnp.clip(r - wr // 2, 0, rows - wr)
        k_band = lax.dynamic_slice_in_dim(kg, r0, wr, axis=1)
        v_band = lax.dynamic_slice_in_dim(vg, r0, wr, axis=1)
        k_win = k_band[:, :, col_idx]
        v_win = v_band[:, :, col_idx]
        dr_idx = (r0 + jnp.arange(wr)) - r + (NA_ROWS - 1)
        bias = rpb[:, dr_idx[None, :, None], dc_idx[:, None, :]]
        s_loc = jnp.einsum('bqhd,bwqjhd->bhqwj', q_r, k_win).astype(jnp.float32) * scale
        s_loc = (s_loc + bias[None].astype(jnp.float32)).reshape(B, H, GRID_W, wr * NA_COLS)
        s_ctx = jnp.einsum('bqhd,bshd->bhqs', q_r, k_ctx).astype(jnp.float32) * scale
        p = jax.nn.softmax(jnp.concatenate([s_loc, s_ctx], axis=-1), axis=-1).astype(v.dtype)
        p_loc = p[..., :wr * NA_COLS].reshape(B, H, GRID_W, wr, NA_COLS)
        p_ctx = p[..., wr * NA_COLS:]
        return (jnp.einsum('bhqwj,bwqjhd->bqhd', p_loc, v_win)
                + jnp.einsum('bhqs,bshd->bqhd', p_ctx, v_ctx))

    o = lax.map(one_row, (jnp.arange(rows), jnp.moveaxis(qg, 1, 0)))
    return jnp.moveaxis(o, 0, 1).reshape(B, T, H, dh)


def split_cols(z):
    idx, acc = [], 0
    for w in IN_WIDTHS[:-1]:
        acc += w
        idx.append(acc)
    return jnp.split(z, idx, axis=-1)


def mixer_project(h, lp):
    B, T, _ = h.shape
    cq, ckv, kr, qb, kb, vb, qc, kc, vc, ga, gb, gc = split_cols(h @ lp['w_in'])
    q_a = (rms_norm(cq, lp['a_q_norm']) @ lp['w_uq']).reshape(B, T, H_A, NOPE_A + ROPE_A)
    ckv = rms_norm(ckv, lp['a_kv_norm'])
    q_b = rms_norm(qb.reshape(B, T, H_B, HD_B), lp['b_q_norm'])
    k_b = rms_norm(kb.reshape(B, T, KVH_B, HD_B), lp['b_k_norm'])
    v_b = vb.reshape(B, T, KVH_B, HD_B)
    q_c = qc.reshape(B, T, H_C, HD_C)
    k_c = kc.reshape(B, T, H_C, HD_C)
    v_c = vc.reshape(B, T, H_C, HD_C)
    gates = (jax.nn.sigmoid(ga), jax.nn.sigmoid(gb), jax.nn.sigmoid(gc))
    return q_a, ckv, kr, q_b, k_b, v_b, q_c, k_c, v_c, gates


def expand_mla(ckv, w_ukv):
    B, S, _ = ckv.shape
    kv = (ckv @ w_ukv).reshape(B, S, H_A, NOPE_A + V_A)
    return kv[..., :NOPE_A], kv[..., NOPE_A:]


def mla_keys(k_nope, k_rope):
    B, S, H, _ = k_nope.shape
    return jnp.concatenate([k_nope, jnp.broadcast_to(k_rope[:, :, None, :], (B, S, H, ROPE_A))], axis=-1)


def merge_branches(o_a, o_b, o_c, gates, lp):
    B, T = o_a.shape[:2]
    g_a, g_b, g_c = gates
    m = (g_a * (o_a.reshape(B, T, H_A * V_A) @ lp['w_bo_a'])
         + g_b * (o_b.reshape(B, T, H_B * HD_B) @ lp['w_bo_b'])
         + g_c * (o_c.reshape(B, T, H_C * HD_C) @ lp['w_bo_c']))
    return m @ lp['w_o']


def peer_ffn(h, lp):
    B, T, D = h.shape
    u, v = lp['peer_u'], lp['peer_v']
    q = (h @ lp['peer_wq']).reshape(B, T, P_HEADS, 2, D_KEY // 2).astype(jnp.float32)
    s1 = jnp.einsum('bthd,hkd->bthk', q[..., 0, :], lp['peer_k1'].astype(jnp.float32))
    s2 = jnp.einsum('bthd,hkd->bthk', q[..., 1, :], lp['peer_k2'].astype(jnp.float32))
    v1, i1 = lax.top_k(s1, P_TOPK)
    v2, i2 = lax.top_k(s2, P_TOPK)
    cand = (v1[..., :, None] + v2[..., None, :]).reshape(B, T, P_HEADS, P_TOPK * P_TOPK)
    cidx = (i1[..., :, None] * N_KEYS + i2[..., None, :]).reshape(B, T, P_HEADS, P_TOPK * P_TOPK)
    top_s, pos = lax.top_k(cand, P_TOPK)
    eidx = jnp.take_along_axis(cidx, pos, axis=-1)
    gate = jax.nn.softmax(top_s, axis=-1).astype(h.dtype)
    nc = (B * T) // PEER_CHUNK
    xs = h.reshape(nc, PEER_CHUNK, D)
    es = eidx.reshape(nc, PEER_CHUNK, P_HEADS * P_TOPK)
    gs = gate.reshape(nc, PEER_CHUNK, P_HEADS * P_TOPK)

    def chunk(args):
        xc, ec, gc = args
        act = jax.nn.gelu(jnp.einsum('cd,ced->ce', xc, u[ec]), approximate=False)
        return jnp.einsum('ce,ced->cd', gc * act, v[ec])

    return lax.map(chunk, (xs, es, gs)).reshape(B, T, D)


def modulation(cond, w_ada, b_ada):
    return jnp.split(jax.nn.silu(cond) @ w_ada + b_ada, 6, axis=-1)


def context_layer(x, mods, lp):
    sh1, sc1, g1, sh2, sc2, g2 = mods
    h = x * (1 + sc1) + sh1
    q_a, ckv, kr, q_b, k_b, v_b, q_c, k_c, v_c, gates = mixer_project(h, lp)
    k_nope, v_a = expand_mla(ckv, lp['w_ukv'])
    o_a = block_attention(q_a, mla_keys(k_nope, kr), v_a)
    o_b = block_attention(q_b, k_b, v_b)
    o_c = block_attention(q_c, k_c, v_c)
    x = layer_norm(ALPHA * x + g1 * merge_branches(o_a, o_b, o_c, gates, lp), lp['ln1_g'], lp['ln1_b'])
    h2 = x * (1 + sc2) + sh2
    x = layer_norm(ALPHA * x + g2 * peer_ffn(h2, lp), lp['ln2_g'], lp['ln2_b'])
    return x, (ckv, kr, k_b, v_b, k_c, v_c)


def latent_layer(x, mods, cache, lp):
    sh1, sc1, g1, sh2, sc2, g2 = mods
    ckv_c, kr_c, kb_c, vb_c, kc_c, vc_c = cache
    h = x * (1 + sc1) + sh1
    q_a, ckv, kr, q_b, k_b, v_b, q_c, k_c, v_c, gates = mixer_project(h, lp)
    q_a = jnp.concatenate([q_a[..., :NOPE_A], axial_rope(q_a[..., NOPE_A:])], axis=-1)
    kr = axial_rope(kr[:, :, None, :])[:, :, 0, :]
    kn_l, va_l = expand_mla(ckv, lp['w_ukv'])
    kn_c, va_c = expand_mla(ckv_c, lp['w_ukv'])
    k_a = jnp.concatenate([mla_keys(kn_l, kr), mla_keys(kn_c, kr_c)], axis=1)
    o_a = block_attention(q_a, k_a, jnp.concatenate([va_l, va_c], axis=1))
    k_b_all = jnp.concatenate([axial_rope(k_b), kb_c], axis=1)
    o_b = block_attention(axial_rope(q_b), k_b_all, jnp.concatenate([v_b, vb_c], axis=1))
    o_c = neighbourhood_attention(q_c, k_c, v_c, kc_c, vc_c, lp['c_rpb'])
    x = layer_norm(ALPHA * x + g1 * merge_branches(o_a, o_b, o_c, gates, lp), lp['ln1_g'], lp['ln1_b'])
    h2 = x * (1 + sc2) + sh2
    return layer_norm(ALPHA * x + g2 * peer_ffn(h2, lp), lp['ln2_g'], lp['ln2_b'])


def setup_inputs(seed: int = 0) -> dict:
    key = jax.random.key(seed)
    ks = jax.random.split(key, 40)

    def nrm(k, shape, s):
        return jax.random.normal(k, shape, jnp.float32) * s

    D = D_MODEL
    return {
        'x_prompt': nrm(ks[0], (BATCH, SEQ, D), 1.0),
        'x_sample': nrm(ks[1], (DEC_BATCH, DEC_SEQ, D), 1.0),
        'cache_a_ckv': nrm(ks[2], (DEC_BATCH, DEPTH, PAST_LEN, KV_LORA), 1.0),
        'cache_a_krope': nrm(ks[3], (DEC_BATCH, DEPTH, PAST_LEN, ROPE_A), 1.0),
        'cache_b_k': nrm(ks[4], (DEC_BATCH, DEPTH, PAST_LEN, KVH_B, HD_B), 1.0),
        'cache_b_v': nrm(ks[5], (DEC_BATCH, DEPTH, PAST_LEN, KVH_B, HD_B), 1.0),
        'cache_c_k': nrm(ks[6], (DEC_BATCH, DEPTH, PAST_LEN, H_C, HD_C), 1.0),
        'cache_c_v': nrm(ks[7], (DEC_BATCH, DEPTH, PAST_LEN, H_C, HD_C), 1.0),
        'c': nrm(ks[8], (DEC_BATCH, D), 1.0),
        'c_ctx': nrm(ks[9], (D,), 1.0),
        'w_ada': nrm(ks[10], (DEPTH, D, 6 * D), 0.5 * D ** -0.5),
        'b_ada': nrm(ks[11], (DEPTH, 6 * D), 0.01),
        'w_in': nrm(ks[12], (DEPTH, D, N_IN), D ** -0.5),
        'a_q_norm': 1.0 + nrm(ks[13], (DEPTH, Q_LORA), 0.02),
        'a_kv_norm': 1.0 + nrm(ks[14], (DEPTH, KV_LORA), 0.02),
        'w_uq': nrm(ks[15], (DEPTH, Q_LORA, H_A * (NOPE_A + ROPE_A)), Q_LORA ** -0.5),
        'w_ukv': nrm(ks[16], (DEPTH, KV_LORA, H_A * (NOPE_A + V_A)), KV_LORA ** -0.5),
        'b_q_norm': 1.0 + nrm(ks[17], (DEPTH, HD_B), 0.02),
        'b_k_norm': 1.0 + nrm(ks[18], (DEPTH, HD_B), 0.02),
        'c_rpb': nrm(ks[19], (DEPTH, H_C, 2 * NA_ROWS - 1, 2 * NA_COLS - 1), 0.5),
        'w_bo_a': nrm(ks[20], (DEPTH, H_A * V_A, D), (H_A * V_A) ** -0.5),
        'w_bo_b': nrm(ks[21], (DEPTH, H_B * HD_B, D), (H_B * HD_B) ** -0.5),
        'w_bo_c': nrm(ks[22], (DEPTH, H_C * HD_C, D), (H_C * HD_C) ** -0.5),
        'w_o': nrm(ks[23], (DEPTH, D, D), BETA * D ** -0.5),
        'ln1_g': 1.0 + nrm(ks[24], (DEPTH, D), 0.02),
        'ln1_b': nrm(ks[25], (DEPTH, D), 0.02),
        'ln2_g': 1.0 + nrm(ks[26], (DEPTH, D), 0.02),
        'ln2_b': nrm(ks[27], (DEPTH, D), 0.02),
        'peer_wq': nrm(ks[28], (DEPTH, D, P_HEADS * D_KEY), D ** -0.5),
        'peer_k1': nrm(ks[29], (DEPTH, P_HEADS, N_KEYS, D_KEY // 2), (D_KEY // 2) ** -0.5),
        'peer_k2': nrm(ks[30], (DEPTH, P_HEADS, N_KEYS, D_KEY // 2), (D_KEY // 2) ** -0.5),
        'peer_u': nrm(ks[31], (DEPTH, N_EXPERTS, D), D ** -0.5),
        'peer_v': nrm(ks[32], (DEPTH, N_EXPERTS, D), BETA),
    }


def reference(x_prompt, x_sample, cache_a_ckv, cache_a_krope, cache_b_k, cache_b_v, cache_c_k, cache_c_v,
              c, c_ctx, w_ada, b_ada, w_in, a_q_norm, a_kv_norm, w_uq, w_ukv, b_q_norm, b_k_norm, c_rpb,
              w_bo_a, w_bo_b, w_bo_c, w_o, ln1_g, ln1_b, ln2_g, ln2_b,
              peer_wq, peer_k1, peer_k2, peer_u, peer_v):
    xp, xs = x_prompt, x_sample
    n_ckv, n_kr, n_bk, n_bv, n_ck, n_cv = [], [], [], [], [], []
    for l in range(DEPTH):
        lp = dict(w_in=w_in[l], a_q_norm=a_q_norm[l], a_kv_norm=a_kv_norm[l], w_uq=w_uq[l], w_ukv=w_ukv[l],
                  b_q_norm=b_q_norm[l], b_k_norm=b_k_norm[l], c_rpb=c_rpb[l],
                  w_bo_a=w_bo_a[l], w_bo_b=w_bo_b[l], w_bo_c=w_bo_c[l], w_o=w_o[l],
                  ln1_g=ln1_g[l], ln1_b=ln1_b[l], ln2_g=ln2_g[l], ln2_b=ln2_b[l],
                  peer_wq=peer_wq[l], peer_k1=peer_k1[l], peer_k2=peer_k2[l],
                  peer_u=peer_u[l], peer_v=peer_v[l])
        mods_ctx = modulation(c_ctx, w_ada[l], b_ada[l])
        xp, (ckv, kr, kb, vb, kc, vc) = context_layer(xp, mods_ctx, lp)
        n_ckv.append(ckv); n_kr.append(kr); n_bk.append(kb); n_bv.append(vb); n_ck.append(kc); n_cv.append(vc)
        mods_lat = [m[:, None, :] for m in modulation(c, w_ada[l], b_ada[l])]
        cache_l = (cache_a_ckv[:, l], cache_a_krope[:, l], cache_b_k[:, l], cache_b_v[:, l],
                   cache_c_k[:, l], cache_c_v[:, l])
        xs = latent_layer(xs, mods_lat, cache_l, lp)
    new_a_ckv = jnp.stack(n_ckv, axis=1)
    new_a_krope = jnp.stack(n_kr, axis=1)
    new_b_k = jnp.stack(n_bk, axis=1)
    new_b_v = jnp.stack(n_bv, axis=1)
    new_c_k = jnp.stack(n_ck, axis=1)
    new_c_v = jnp.stack(n_cv, axis=1)
    return (xp, xs, new_a_ckv, new_a_krope, new_b_k, new_b_v, new_c_k, new_c_v)
```

```python
import functools

import jax
import jax.numpy as jnp
import numpy as np
from jax import lax
from jax.experimental import pallas as pl
from jax.experimental.pallas import tpu as pltpu

F32 = jnp.float32
BF16 = jnp.bfloat16

GRID_W = 64
NA_ROWS = 8
NA_COLS = 16
ROPE_BASE = 10000.0
EPS = 1e-6
NOPE_A = 128
ROPE_A = 64
V_A = 128
HD_B = 128
HD_C = 128
P_TOPK = 16

LANES = 128
SUBLANES = 8
VMEM_LIMIT = 56 * 1024 * 1024
NEG = -1e30
INV_SQRT2 = 0.7071067811865476

PEER_TT = 512
PEER_TE = 512


def _cparams(*sem):
    return pltpu.CompilerParams(dimension_semantics=sem, vmem_limit_bytes=VMEM_LIMIT)


def _resident(shape):
    nd = len(shape)
    return pl.BlockSpec(shape, lambda *_: (0,) * nd, pipeline_mode=pl.Buffered(1))


def _dot(a, b):
    return jnp.dot(a, b, preferred_element_type=F32)


def _dot_nt(a, b):
    return lax.dot_general(a, b, (((1,), (1,)), ((), ())), preferred_element_type=F32)


def _dot_tn(a, b):
    return lax.dot_general(a, b, (((0,), (0,)), ((), ())), preferred_element_type=F32)


def _split(x):
    hi = x.astype(BF16)
    return hi, (x - hi.astype(F32)).astype(BF16)


def _dot3(a, b):
    ah, al = _split(a)
    bh, bl = _split(b)
    return _dot(ah, bh) + _dot(al, bh) + _dot(ah, bl)


def _rms(x, g):
    return (x * lax.rsqrt(jnp.mean(x * x, axis=-1, keepdims=True) + EPS)) * g


def _layer_norm(v, g, b):
    mu = jnp.mean(v, axis=-1, keepdims=True)
    d = v - mu
    var = jnp.mean(d * d, axis=-1, keepdims=True)
    return d * lax.rsqrt(var + EPS) * g + b


def _softmax_parts(parts):
    m = functools.reduce(jnp.maximum, [jnp.max(s, axis=-1, keepdims=True) for s in parts])
    es = [jnp.exp(s - m) for s in parts]
    z = functools.reduce(jnp.add, [jnp.sum(e, axis=-1, keepdims=True) for e in es])
    return [e / z for e in es]


def _mods_kernel(c_ref, w_ref, b_ref, o_ref):
    c = c_ref[...]
    o_ref[...] = _dot3(c * jax.nn.sigmoid(c), w_ref[...]) + b_ref[...]


def _modulations(cond, w_ada, b_ada):
    L, D, N = w_ada.shape
    tn = 1024
    out = pl.pallas_call(
        _mods_kernel,
        grid=(L, N // tn),
        in_specs=[pl.BlockSpec((SUBLANES, D), lambda l, j: (0, 0)),
                  pl.BlockSpec((None, D, tn), lambda l, j: (l, 0, j)),
                  pl.BlockSpec((None, 1, tn), lambda l, j: (l, 0, j))],
        out_specs=pl.BlockSpec((None, SUBLANES, tn), lambda l, j: (l, 0, j)),
        out_shape=jax.ShapeDtypeStruct((L, SUBLANES, N), F32),
        compiler_params=_cparams("parallel", "parallel"),
        name="mods",
    )(cond, w_ada, b_ada.reshape(L, 1, N))
    return out.reshape(L, SUBLANES, 6, D)


class _Group:
    def __init__(self, n_batch, seq, row_base, tokens_per_row, rope):
        self.n_batch, self.seq = n_batch, seq
        self.m = n_batch * seq
        self.row_base, self.tokens_per_row = row_base, tokens_per_row
        self.rope = rope

    def mod_spec(self, tm, d):
        assert self.tokens_per_row % tm == 0
        per = self.tokens_per_row // tm
        base = self.row_base
        return pl.BlockSpec((None, 6, d), lambda i, *_: (base + i // per, 0, 0))

    def pos_spec(self, tm, w):
        assert self.seq % tm == 0
        per = self.seq // tm
        return pl.BlockSpec((tm, w), lambda i, *_: (i % per, 0))


def _row_spec(tm, w):
    return pl.BlockSpec((tm, w), lambda i, *_: (i, 0))


def _modulate_kernel(x_ref, m_ref, o_ref):
    o_ref[...] = (x_ref[...] * (1.0 + m_ref[1:2, :]) + m_ref[0:1, :]).astype(o_ref.dtype)


def _modulate(x, mods_l, grp, tm=512):
    M, D = x.shape
    return pl.pallas_call(
        _modulate_kernel,
        grid=(M // tm,),
        in_specs=[_row_spec(tm, D), grp.mod_spec(tm, D)],
        out_specs=_row_spec(tm, D),
        out_shape=jax.ShapeDtypeStruct((M, D), BF16),
        compiler_params=_cparams("parallel"),
        name="modulate",
    )(x, mods_l)


def _inproj_a_kernel(h_ref, w_ref, gq_ref, gkv_ref, *rest, ql, kl, rope):
    if rope:
        cos_ref, sin_ref, cq_o, ckv_o, kr_o, krp_o = rest
    else:
        cq_o, ckv_o, kr_o, krp_o = rest
    z = _dot(h_ref[...], w_ref[...])
    cq_o[...] = _rms(z[:, :ql], gq_ref[...]).astype(cq_o.dtype)
    ckv_o[...] = _rms(z[:, ql:ql + kl], gkv_ref[...])
    o = ql + kl
    kr = z[:, o:o + LANES]
    if rope:
        kr = kr * cos_ref[...] + z[:, o + LANES:o + 2 * LANES] * sin_ref[...]
    kr_o[...] = kr[:, :ROPE_A]
    krp_o[...] = kr.astype(krp_o.dtype)


def _inproj_a(h, w_a, gq, gkv, grp, tabs, tm=512):
    M, D = h.shape
    ql, kl = gq.shape[1], gkv.shape[1]
    wn = ql + kl + (2 * LANES if grp.rope else LANES)
    in_specs = [_row_spec(tm, D), pl.BlockSpec((D, wn), lambda i: (0, 0), pipeline_mode=pl.Buffered(1)),
                _resident((1, ql)), _resident((1, kl))]
    args = [h, w_a, gq, gkv]
    if grp.rope:
        in_specs += [grp.pos_spec(tm, LANES), grp.pos_spec(tm, LANES)]
        args += [tabs["kr_cos"], tabs["kr_sin"]]
    return pl.pallas_call(
        functools.partial(_inproj_a_kernel, ql=ql, kl=kl, rope=grp.rope),
        grid=(M // tm,),
        in_specs=in_specs,
        out_specs=[_row_spec(tm, ql), _row_spec(tm, kl), _row_spec(tm, ROPE_A), _row_spec(tm, LANES)],
        out_shape=[jax.ShapeDtypeStruct((M, ql), BF16), jax.ShapeDtypeStruct((M, kl), F32),
                   jax.ShapeDtypeStruct((M, ROPE_A), F32), jax.ShapeDtypeStruct((M, LANES), BF16)],
        compiler_params=_cparams("parallel"),
        name="inproj_a",
    )(*args)


def _inproj_b_kernel(h_ref, w_ref, gq_ref, gk_ref, *rest, hq, hk, rope):
    if rope:
        gqs_ref, gks_ref, cos_ref, sin_ref, q_o, k_o, v_o = rest
    else:
        q_o, k_o, v_o = rest
    z = _dot(h_ref[...], w_ref[...])
    nq, nk = hq * HD_B, hk * HD_B
    sw = nq + 2 * nk

    def head(col, g_ref, sw_col, gs_ref):
        zh = z[:, col:col + HD_B]
        r = lax.rsqrt(jnp.mean(zh * zh, axis=-1, keepdims=True) + EPS)
        y = (zh * r) * g_ref[...]
        if rope:
            ys = (z[:, sw_col:sw_col + HD_B] * r) * gs_ref[...]
            y = y * cos_ref[...] + ys * sin_ref[...]
        return y

    for i in range(hq):
        q_o[:, i * HD_B:(i + 1) * HD_B] = head(
            i * HD_B, gq_ref, sw + i * HD_B, gqs_ref if rope else None).astype(q_o.dtype)
    for i in range(hk):
        k_o[:, i * HD_B:(i + 1) * HD_B] = head(
            nq + i * HD_B, gk_ref, sw + nq + i * HD_B, gks_ref if rope else None).astype(k_o.dtype)
    v_o[...] = z[:, nq + nk:nq + 2 * nk].astype(v_o.dtype)


def _inproj_b(h, w_b, gq, gk, gqs, gks, hq, hk, grp, tabs, kv_dtype, tm=512):
    M, D = h.shape
    nq, nk = hq * HD_B, hk * HD_B
    wn = nq + 2 * nk + (nq + nk if grp.rope else 0)
    in_specs = [_row_spec(tm, D), pl.BlockSpec((D, wn), lambda i: (0, 0), pipeline_mode=pl.Buffered(1)),
                _resident((1, HD_B)), _resident((1, HD_B))]
    args = [h, w_b, gq, gk]
    if grp.rope:
        in_specs += [_resident((1, HD_B)), _resident((1, HD_B)),
                     grp.pos_spec(tm, HD_B), grp.pos_spec(tm, HD_B)]
        args += [gqs, gks, tabs["b_cos"], tabs["b_sin"]]
    return pl.pallas_call(
        functools.partial(_inproj_b_kernel, hq=hq, hk=hk, rope=grp.rope),
        grid=(M // tm,),
        in_specs=in_specs,
        out_specs=[_row_spec(tm, nq), _row_spec(tm, nk), _row_spec(tm, nk)],
        out_shape=[jax.ShapeDtypeStruct((M, nq), BF16), jax.ShapeDtypeStruct((M, nk), kv_dtype),
                   jax.ShapeDtypeStruct((M, nk), kv_dtype)],
        compiler_params=_cparams("parallel"),
        name="inproj_b",
    )(*args)


def _mm_kernel(a_ref, w_ref, o_ref, *, act):
    z = _dot(a_ref[...].astype(BF16), w_ref[...])
    if act == "sigmoid":
        z = jax.nn.sigmoid(z)
    o_ref[...] = z.astype(o_ref.dtype)


def _mm(a, w, out_dtype, act=None, tm=512, tn=512, col0=0, ncols=None, name="mm"):
    M, K = a.shape
    ncols = w.shape[1] - col0 if ncols is None else ncols
    assert col0 % tn == 0 and ncols % tn == 0
    cb = col0 // tn
    return pl.pallas_call(
        functools.partial(_mm_kernel, act=act),
        grid=(M // tm, ncols // tn),
        in_specs=[pl.BlockSpec((tm, K), lambda i, j: (i, 0)),
                  pl.BlockSpec((K, tn), lambda i, j: (0, cb + j))],
        out_specs=pl.BlockSpec((tm, tn), lambda i, j: (i, j)),
        out_shape=jax.ShapeDtypeStruct((M, ncols), out_dtype),
        compiler_params=_cparams("parallel", "arbitrary"),
        name=name,
    )(a, w)


def _qa_kernel(c_ref, w_ref, *rest, nope_w, rope):
    if rope:
        cos_ref, sin_ref, o_ref = rest
    else:
        (o_ref,) = rest
    z = _dot(c_ref[...], w_ref[...])
    o_ref[:, :nope_w] = z[:, :nope_w].astype(o_ref.dtype)
    r = z[:, nope_w:2 * nope_w]
    if rope:
        r = r * cos_ref[...] + z[:, 2 * nope_w:3 * nope_w] * sin_ref[...]
    o_ref[:, nope_w:] = r.astype(o_ref.dtype)


def _qa_proj(cq, w_uq_p, ha, grp, tabs, tm=512):
    M, ql = cq.shape
    nope_w = ha * LANES
    wn = (3 if grp.rope else 2) * nope_w
    in_specs = [_row_spec(tm, ql), pl.BlockSpec((ql, wn), lambda i: (0, 0), pipeline_mode=pl.Buffered(1))]
    args = [cq, w_uq_p]
    if grp.rope:
        in_specs += [grp.pos_spec(tm, nope_w), grp.pos_spec(tm, nope_w)]
        args += [tabs["qa_cos"], tabs["qa_sin"]]
    return pl.pallas_call(
        functools.partial(_qa_kernel, nope_w=nope_w, rope=grp.rope),
        grid=(M // tm,),
        in_specs=in_specs,
        out_specs=_row_spec(tm, 2 * nope_w),
        out_shape=jax.ShapeDtypeStruct((M, 2 * nope_w), BF16),
        compiler_params=_cparams("parallel"),
        name="qa_proj",
    )(*args)


def _attn_kernel(qa_ref, kva_ref, kr_ref, qb_ref, kb_ref, vb_ref, *rest, ha, hb, gb, hc):
    if hc:
        qc_ref, kc_ref, vc_ref, o_ref = rest
    else:
        (o_ref,) = rest
    nw = ha * LANES
    kr = kr_ref[...].astype(BF16)
    scale_a = (NOPE_A + ROPE_A) ** -0.5
    for h in range(ha):
        s = _dot_nt(qa_ref[:, h * LANES:(h + 1) * LANES], kva_ref[:, h * LANES:(h + 1) * LANES])
        s = s + _dot_nt(qa_ref[:, nw + h * LANES:nw + (h + 1) * LANES], kr)
        (p,) = _softmax_parts([s * scale_a])
        o = _dot(p.astype(BF16), kva_ref[:, nw + h * V_A:nw + (h + 1) * V_A])
        o_ref[:, h * V_A:(h + 1) * V_A] = o.astype(o_ref.dtype)
    off = ha * V_A
    rep = hb // gb
    for h in range(hb):
        g = h // rep
        s = _dot_nt(qb_ref[:, h * HD_B:(h + 1) * HD_B], kb_ref[:, g * HD_B:(g + 1) * HD_B].astype(BF16))
        (p,) = _softmax_parts([s * HD_B ** -0.5])
        o = _dot(p.astype(BF16), vb_ref[:, g * HD_B:(g + 1) * HD_B].astype(BF16))
        o_ref[:, off + h * HD_B:off + (h + 1) * HD_B] = o.astype(o_ref.dtype)
    off += hb * HD_B
    for h in range(hc):
        s = _dot_nt(qc_ref[:, h * HD_C:(h + 1) * HD_C], kc_ref[:, h * HD_C:(h + 1) * HD_C].astype(BF16))
        (p,) = _softmax_parts([s * HD_C ** -0.5])
        o = _dot(p.astype(BF16), vc_ref[:, h * HD_C:(h + 1) * HD_C].astype(BF16))
        o_ref[:, off + h * HD_C:off + (h + 1) * HD_C] = o.astype(o_ref.dtype)


def _attention(qa, kva, kr, qb, kb, vb, c_parts, ha, hb, gb, hc, tq=256):
    B, T, _ = qa.shape

    def qspec(a):
        return pl.BlockSpec((None, tq, a.shape[2]), lambda b, i: (b, i, 0))

    def kspec(a):
        return pl.BlockSpec((None, a.shape[1], a.shape[2]), lambda b, i: (b, 0, 0))

    args = [qa, kva, kr, qb, kb, vb]
    in_specs = [qspec(qa), kspec(kva), kspec(kr), qspec(qb), kspec(kb), kspec(vb)]
    if hc:
        qc, kc, vc = c_parts
        args += [qc, kc, vc]
        in_specs += [qspec(qc), kspec(kc), kspec(vc)]
    wo = ha * V_A + hb * HD_B + hc * HD_C
    return pl.pallas_call(
        functools.partial(_attn_kernel, ha=ha, hb=hb, gb=gb, hc=hc),
        grid=(B, T // tq),
        in_specs=in_specs,
        out_specs=pl.BlockSpec((None, tq, wo), lambda b, i: (b, i, 0)),
        out_shape=jax.ShapeDtypeStruct((B, T, wo), BF16),
        compiler_params=_cparams("parallel", "parallel"),
        name="attention",
    )(*args)


def _na_kernel(q_ref, k_ref, v_ref, kc_ref, vc_ref, bias_ref, o_ref, *, hc, rows, wr):
    r = pl.program_id(1)
    r0 = jnp.clip(r - wr // 2, 0, rows - wr)
    start = pl.multiple_of(r0 * GRID_W, GRID_W)
    band = pl.ds(start, wr * GRID_W)
    scale = HD_C ** -0.5
    for h in range(hc):
        sl = slice(h * HD_C, (h + 1) * HD_C)
        q = q_ref[:, sl]
        s_loc = _dot_nt(q, k_ref[band, sl]) * scale + bias_ref[h]
        s_ctx = _dot_nt(q, kc_ref[:, sl].astype(BF16)) * scale
        p_loc, p_ctx = _softmax_parts([s_loc, s_ctx])
        o = _dot(p_loc.astype(BF16), v_ref[band, sl]) + _dot(p_ctx.astype(BF16), vc_ref[:, sl].astype(BF16))
        o_ref[:, sl] = o.astype(o_ref.dtype)


def _na_attention(q, k, v, kc, vc, bias, hc):
    B, T, W = q.shape
    rows = T // GRID_W
    wr = min(NA_ROWS, rows)
    L = kc.shape[1]

    def delta(r):
        return r - jnp.clip(r - wr // 2, 0, rows - wr)

    return pl.pallas_call(
        functools.partial(_na_kernel, hc=hc, rows=rows, wr=wr),
        grid=(B, rows),
        in_specs=[pl.BlockSpec((None, GRID_W, W), lambda b, r: (b, r, 0)),
                  pl.BlockSpec((None, T, W), lambda b, r: (b, 0, 0)),
                  pl.BlockSpec((None, T, W), lambda b, r: (b, 0, 0)),
                  pl.BlockSpec((None, L, W), lambda b, r: (b, 0, 0)),
                  pl.BlockSpec((None, L, W), lambda b, r: (b, 0, 0)),
                  pl.BlockSpec((hc, None, GRID_W, wr * GRID_W), lambda b, r: (0, delta(r), 0, 0))],
        out_specs=pl.BlockSpec((None, GRID_W, W), lambda b, r: (b, r, 0)),
        out_shape=jax.ShapeDtypeStruct((B, T, W), BF16),
        compiler_params=_cparams("parallel", "arbitrary"),
        name="na_attention",
    )(q, k, v, kc, vc, bias)


def _merge_kernel(oa_ref, ob_ref, oc_ref, g_ref, wa_ref, wb_ref, wc_ref, m_ref, *, d):
    m = g_ref[:, :d] * _dot(oa_ref[...], wa_ref[...])
    m = m + g_ref[:, d:2 * d] * _dot(ob_ref[...], wb_ref[...])
    m = m + g_ref[:, 2 * d:] * _dot(oc_ref[...], wc_ref[...])
    m_ref[...] = m.astype(m_ref.dtype)


def _merge(branches, gates, wa, wb, wc, tm=256):
    M = gates.shape[0]
    D = wa.shape[1]

    def col_spec(off, w):
        assert off % w == 0
        return pl.BlockSpec((tm, w), lambda i: (i, off // w))

    return pl.pallas_call(
        functools.partial(_merge_kernel, d=D),
        grid=(M // tm,),
        in_specs=[col_spec(off, w.shape[0]) for (_, off), w in zip(branches, (wa, wb, wc))]
        + [_row_spec(tm, 3 * D), _resident(wa.shape), _resident(wb.shape), _resident(wc.shape)],
        out_specs=_row_spec(tm, D),
        out_shape=jax.ShapeDtypeStruct((M, D), BF16),
        compiler_params=_cparams("parallel"),
        name="merge",
    )(*[a for a, _ in branches], gates, wa, wb, wc)


def _oproj_kernel(m_ref, w_ref, x_ref, mod_ref, g_ref, b_ref, x1_ref, h2_ref, *, alpha):
    r = _dot(m_ref[...], w_ref[...])
    x1 = _layer_norm(alpha * x_ref[...] + mod_ref[2:3, :] * r, g_ref[...], b_ref[...])
    x1_ref[...] = x1
    h2_ref[...] = (x1 * (1.0 + mod_ref[4:5, :]) + mod_ref[3:4, :]).astype(h2_ref.dtype)


def _oproj(m, w_o, x, mods_l, ln_g, ln_b, alpha, grp, tm=256):
    M, D = x.shape
    return pl.pallas_call(
        functools.partial(_oproj_kernel, alpha=alpha),
        grid=(M // tm,),
        in_specs=[_row_spec(tm, D), _resident(w_o.shape), _row_spec(tm, D), grp.mod_spec(tm, D),
                  _resident((1, D)), _resident((1, D))],
        out_specs=[_row_spec(tm, D), _row_spec(tm, D)],
        out_shape=[jax.ShapeDtypeStruct((M, D), F32), jax.ShapeDtypeStruct((M, D), BF16)],
        compiler_params=_cparams("parallel"),
        name="oproj_ln",
    )(m, w_o, x, mods_l, ln_g, ln_b)


PEER_ROWS = 4 * LANES + SUBLANES
_PAIR_ROWS = tuple(P_TOPK // (a + 1) for a in range(SUBLANES))


def _peer_scores_kernel(h_ref, wq_ref, k1_ref, k2_ref, o_ref, top_scr, *, ph, tt, nk):
    h2 = h_ref[...]
    neg_inf = -jnp.inf
    row_id = lax.broadcasted_iota(jnp.int32, (SUBLANES, LANES), 0)

    def head(hh, carry):
        for c, k_ref in ((0, k1_ref), (1, k2_ref)):
            start = pl.multiple_of(hh * (2 * nk) + c * nk, nk)
            q_t = _dot_nt(wq_ref[pl.ds(start, nk), :], h2)
            o_ref[hh, c * nk:(c + 1) * nk, :] = _dot3(k_ref[hh], q_t)
        for lc in range(tt // LANES):
            sl = slice(lc * LANES, (lc + 1) * LANES)
            for c in range(2):
                w = o_ref[hh, c * nk:(c + 1) * nk, sl]
                for k in range(P_TOPK):
                    m = jnp.max(w, axis=0, keepdims=True)
                    top_scr[c * P_TOPK + k:c * P_TOPK + k + 1, :] = m
                    w = jnp.where(w == m, neg_inf, w)
            v1_0 = top_scr[0:1, :]
            v2_0 = top_scr[P_TOPK:P_TOPK + 1, :]
            v2a = top_scr[P_TOPK:P_TOPK + SUBLANES, :]
            v2b = top_scr[P_TOPK + SUBLANES:2 * P_TOPK, :]
            blocks = [v1_0 + v2a, v1_0 + v2b]
            for a in range(1, SUBLANES):
                blk = top_scr[a:a + 1, :] + v2a
                if _PAIR_ROWS[a] < SUBLANES:
                    blk = jnp.where(row_id < _PAIR_ROWS[a], blk, neg_inf)
                blocks.append(blk)
            blocks.append(top_scr[SUBLANES:P_TOPK, :] + v2_0)
            m0 = None
            z = None
            for k in range(P_TOPK):
                m = functools.reduce(jnp.maximum, blocks)
                m = jnp.max(m, axis=0, keepdims=True)
                if k == 0:
                    m0 = m
                    z = jnp.ones_like(m)
                else:
                    z = z + jnp.exp(m - m0)
                if k + 1 < P_TOPK:
                    blocks = [jnp.where(b == m, neg_inf, b) for b in blocks]
            o_ref[hh, 2 * nk:3 * nk, sl] = jnp.exp(o_ref[hh, 0:nk, sl] - v1_0)
            o_ref[hh, 3 * nk:4 * nk, sl] = jnp.exp(o_ref[hh, nk:2 * nk, sl] - v2_0) * (1.0 / z)
            o_ref[hh, 4 * nk:4 * nk + SUBLANES, sl] = jnp.broadcast_to(m, (SUBLANES, LANES))
        return carry

    lax.fori_loop(0, ph, head, 0)


def _peer_scores(h2, wq_t, k1, k2):
    M, D = h2.shape
    ph, nk, _ = k1.shape
    assert nk == LANES
    tt = PEER_TT
    return pl.pallas_call(
        functools.partial(_peer_scores_kernel, ph=ph, tt=tt, nk=nk),
        grid=(M // tt,),
        in_specs=[_row_spec(tt, D), _resident(wq_t.shape), _resident(k1.shape), _resident(k2.shape)],
        out_specs=pl.BlockSpec((ph, PEER_ROWS, tt), lambda i: (0, 0, i)),
        out_shape=jax.ShapeDtypeStruct((ph, PEER_ROWS, M), F32),
        scratch_shapes=[pltpu.VMEM((2 * P_TOPK, LANES), F32)],
        compiler_params=_cparams("parallel"),
        name="peer_scores",
    )(h2, wq_t, k1, k2)


def _peer_dense_kernel(h_ref, u_ref, v_ref, s_ref, y_ref, act_scr, p_scr, *, ph, tt, te, nk):
    j = pl.program_id(1)

    @pl.when(j == 0)
    def _():
        y_ref[...] = jnp.zeros_like(y_ref)

    act_scr[...] = _dot_nt(u_ref[...], h_ref[...])
    groups = te // nk
    assert SUBLANES % groups == 0
    base = pl.multiple_of((j * groups) // SUBLANES * SUBLANES, SUBLANES)
    sub = (j * groups) % SUBLANES

    def pick(blk, g):
        row = blk[g:g + 1, :]
        for k in range(1, SUBLANES // groups):
            row = jnp.where(sub == k * groups, blk[k * groups + g:k * groups + g + 1, :], row)
        return row

    for g in range(groups):
        for lc in range(tt // LANES):
            sl = slice(lc * LANES, (lc + 1) * LANES)
            gate = jnp.zeros((nk, LANES), F32)
            for hh in range(ph):
                s1 = pick(s_ref[hh, pl.ds(base, SUBLANES), sl], g)
                e1 = pick(s_ref[hh, pl.ds(2 * nk + base, SUBLANES), sl], g)
                tau = s_ref[hh, 4 * nk:4 * nk + 1, sl]
                s2 = s_ref[hh, nk:2 * nk, sl]
                e2 = s_ref[hh, 3 * nk:4 * nk, sl]
                gate = gate + jnp.where(s1 + s2 >= tau, e1 * e2, 0.0)
            a = act_scr[g * nk:(g + 1) * nk, sl]
            gelu = 0.5 * a * (1.0 + lax.erf(a * INV_SQRT2))
            p_scr[g * nk:(g + 1) * nk, sl] = (gate * gelu).astype(p_scr.dtype)
    y_ref[...] += _dot_tn(p_scr[...], v_ref[...])


def _peer_dense(h2, u, v, scores):
    M, D = h2.shape
    E = u.shape[0]
    ph = scores.shape[0]
    tt, te, nk = PEER_TT, PEER_TE, LANES
    return pl.pallas_call(
        functools.partial(_peer_dense_kernel, ph=ph, tt=tt, te=te, nk=nk),
        grid=(M // tt, E // te),
        in_specs=[pl.BlockSpec((tt, D), lambda i, j: (i, 0)),
                  pl.BlockSpec((te, D), lambda i, j: (j, 0)),
                  pl.BlockSpec((te, D), lambda i, j: (j, 0)),
                  pl.BlockSpec((ph, PEER_ROWS, tt), lambda i, j: (0, 0, i))],
        out_specs=pl.BlockSpec((tt, D), lambda i, j: (i, 0)),
        out_shape=jax.ShapeDtypeStruct((M, D), F32),
        scratch_shapes=[pltpu.VMEM((te, tt), F32), pltpu.VMEM((te, tt), BF16)],
        compiler_params=_cparams("parallel", "arbitrary"),
        name="peer_dense",
    )(h2, u, v, scores)


def _ln2_kernel(x_ref, y_ref, mod_ref, g_ref, b_ref, o_ref, *, alpha):
    o_ref[...] = _layer_norm(alpha * x_ref[...] + mod_ref[5:6, :] * y_ref[...], g_ref[...], b_ref[...])


def _ln2(x1, y, mods_l, ln_g, ln_b, alpha, grp, tm=512):
    M, D = x1.shape
    return pl.pallas_call(
        functools.partial(_ln2_kernel, alpha=alpha),
        grid=(M // tm,),
        in_specs=[_row_spec(tm, D), _row_spec(tm, D), grp.mod_spec(tm, D), _resident((1, D)), _resident((1, D))],
        out_specs=_row_spec(tm, D),
        out_shape=jax.ShapeDtypeStruct((M, D), F32),
        compiler_params=_cparams("parallel"),
        name="ln2",
    )(x1, y, mods_l, ln_g, ln_b)


def _rope_perm(d):
    nf = d // 4
    p = np.arange(d)
    return np.where((p // nf) % 2 == 0, p + nf, p - nf)


def _rope_tables(seq, d):
    t = jnp.arange(seq)
    row = (t // GRID_W).astype(F32)
    col = (t % GRID_W).astype(F32)
    half = d // 2
    nf = half // 2
    inv = jnp.power(ROPE_BASE, -jnp.arange(nf, dtype=F32) / nf)
    cos_parts, sin_parts = [], []
    for pos in (row, col):
        ang = pos[:, None] * inv[None, :]
        c, s = jnp.cos(ang), jnp.sin(ang)
        cos_parts += [c, c]
        sin_parts += [-s, s]
    return jnp.concatenate(cos_parts, axis=-1), jnp.concatenate(sin_parts, axis=-1)


def _pad_lanes(a, w=LANES):
    return jnp.pad(a, [(0, 0)] * (a.ndim - 1) + [(0, w - a.shape[-1])])


def _na_bias(rpb, rows):
    wr = min(NA_ROWS, rows)
    col = np.arange(GRID_W)
    c0 = np.clip(col - NA_COLS // 2, 0, GRID_W - NA_COLS)
    kc = np.arange(GRID_W)
    valid = (kc[None, :] >= c0[:, None]) & (kc[None, :] < c0[:, None] + NA_COLS)
    dc = np.clip(kc[None, :] - col[:, None] + (NA_COLS - 1), 0, 2 * NA_COLS - 2)
    slabs = []
    for delta in range(wr):
        per_w = []
        for w in range(wr):
            dr = w - delta + (NA_ROWS - 1)
            per_w.append(jnp.where(valid[None], rpb[:, dr][:, dc], NEG))
        slabs.append(jnp.concatenate(per_w, axis=-1))
    return jnp.stack(slabs, axis=1).astype(F32)


def _layer_weights(l, p, dims):
    D, ql, kl, ha, hb, gb, hc = dims
    w_in = p["w_in"][l]
    o_kr = ql + kl
    o_qb = o_kr + ROPE_A
    o_kb = o_qb + hb * HD_B
    o_vb = o_kb + gb * HD_B
    o_qc = o_vb + gb * HD_B
    o_g = o_qc + 3 * hc * HD_C
    perm_a = _rope_perm(ROPE_A)
    perm_b = _rope_perm(HD_B)
    w_kr = w_in[:, o_kr:o_qb]
    w_a = jnp.concatenate([w_in[:, :o_kr], _pad_lanes(w_kr), _pad_lanes(w_kr[:, perm_a])], axis=1)
    w_qb = w_in[:, o_qb:o_kb].reshape(D, hb, HD_B)
    w_kb = w_in[:, o_kb:o_vb].reshape(D, gb, HD_B)
    w_b = jnp.concatenate([w_in[:, o_qb:o_qc], w_qb[:, :, perm_b].reshape(D, -1),
                           w_kb[:, :, perm_b].reshape(D, -1)], axis=1)
    w_uq = p["w_uq"][l].reshape(ql, ha, NOPE_A + ROPE_A)
    uq_r = w_uq[:, :, NOPE_A:]
    w_uq_p = jnp.concatenate([w_uq[:, :, :NOPE_A].reshape(ql, -1), _pad_lanes(uq_r).reshape(ql, -1),
                              _pad_lanes(uq_r[:, :, perm_a]).reshape(ql, -1)], axis=1)
    w_ukv = p["w_ukv"][l].reshape(kl, ha, NOPE_A + V_A)
    w_ukv_p = jnp.concatenate([w_ukv[:, :, :NOPE_A].reshape(kl, -1), w_ukv[:, :, NOPE_A:].reshape(kl, -1)], axis=1)
    return dict(
        w_a=w_a.astype(BF16), w_b=w_b.astype(BF16), w_c=w_in[:, o_qc:o_g].astype(BF16),
        w_g=w_in[:, o_g:].astype(BF16),
        w_uq=w_uq_p.astype(BF16), w_ukv=w_ukv_p.astype(BF16),
        gq=p["a_q_norm"][l][None], gkv=p["a_kv_norm"][l][None],
        bq=p["b_q_norm"][l][None], bk=p["b_k_norm"][l][None],
        bqs=p["b_q_norm"][l][perm_b][None], bks=p["b_k_norm"][l][perm_b][None],
        w_bo_a=p["w_bo_a"][l].astype(BF16), w_bo_b=p["w_bo_b"][l].astype(BF16),
        w_bo_c=p["w_bo_c"][l].astype(BF16), w_o=p["w_o"][l].astype(BF16),
        ln1_g=p["ln1_g"][l][None], ln1_b=p["ln1_b"][l][None],
        ln2_g=p["ln2_g"][l][None], ln2_b=p["ln2_b"][l][None],
        wq_t=p["peer_wq"][l].T.astype(BF16), k1=p["peer_k1"][l], k2=p["peer_k2"][l],
        u=p["peer_u"][l].astype(BF16), v=p["peer_v"][l].astype(BF16),
    )


def _sublayers(x, mods_l, lw, grp, tabs, dims, cache, na_bias, alpha):
    D, ql, kl, ha, hb, gb, hc = dims
    B, T = grp.n_batch, grp.seq
    h = _modulate(x, mods_l, grp)
    cq, ckv, kr, krp = _inproj_a(h, lw["w_a"], lw["gq"], lw["gkv"], grp, tabs)
    kv_dtype = BF16 if grp.rope else F32
    qb, kb, vb = _inproj_b(h, lw["w_b"], lw["bq"], lw["bk"], lw["bqs"], lw["bks"], hb, gb, grp, tabs, kv_dtype)
    wc = hc * HD_C
    qc = _mm(h, lw["w_c"], BF16, tn=wc, col0=0, ncols=wc, name="inproj_qc")
    kc = _mm(h, lw["w_c"], kv_dtype, tn=wc, col0=wc, ncols=wc, name="inproj_kc")
    vc = _mm(h, lw["w_c"], kv_dtype, tn=wc, col0=2 * wc, ncols=wc, name="inproj_vc")
    gates = _mm(h, lw["w_g"], F32, act="sigmoid", tn=1024, name="inproj_gates")
    qa = _qa_proj(cq, lw["w_uq"], ha, grp, tabs)
    kva = _mm(ckv, lw["w_ukv"], BF16, tn=512, name="mla_kv")

    def b3(a):
        return a.reshape(B, T, a.shape[-1])

    if cache is None:
        o = _attention(b3(qa), b3(kva), b3(krp), b3(qb), b3(kb), b3(vb), (b3(qc), b3(kc), b3(vc)),
                       ha, hb, gb, hc)
        o = o.reshape(B * T, -1)
        branches = [(o, 0), (o, ha * V_A), (o, ha * V_A + hb * HD_B)]
    else:
        ckv_c, kr_c, kb_c, vb_c, kc_c, vc_c = cache
        L = ckv_c.shape[1]
        kva_c = _mm(ckv_c.reshape(B * L, kl), lw["w_ukv"], BF16, tm=min(512, B * L), tn=512, name="mla_kv_ctx")
        kva_all = jnp.concatenate([b3(kva), kva_c.reshape(B, L, -1)], axis=1)
        kr_all = jnp.concatenate([b3(krp), _pad_lanes(kr_c).astype(BF16)], axis=1)
        kb_all = jnp.concatenate([b3(kb), kb_c.reshape(B, L, -1).astype(BF16)], axis=1)
        vb_all = jnp.concatenate([b3(vb), vb_c.reshape(B, L, -1).astype(BF16)], axis=1)
        o = _attention(b3(qa), kva_all, kr_all, b3(qb), kb_all, vb_all, None, ha, hb, gb, 0)
        o = o.reshape(B * T, -1)
        oc = _na_attention(b3(qc), b3(kc), b3(vc), kc_c.reshape(B, L, -1), vc_c.reshape(B, L, -1),
                           na_bias, hc).reshape(B * T, -1)
        branches = [(o, 0), (o, ha * V_A), (oc, 0)]
    m = _merge(branches, gates, lw["w_bo_a"], lw["w_bo_b"], lw["w_bo_c"])
    x1, h2 = _oproj(m, lw["w_o"], x, mods_l, lw["ln1_g"], lw["ln1_b"], alpha, grp)
    scores = _peer_scores(h2, lw["wq_t"], lw["k1"], lw["k2"])
    y = _peer_dense(h2, lw["u"], lw["v"], scores)
    x2 = _ln2(x1, y, mods_l, lw["ln2_g"], lw["ln2_b"], alpha, grp)
    return x2, (ckv, kr, kb, vb, kc, vc)


def kernel(x_prompt, x_sample, cache_a_ckv, cache_a_krope, cache_b_k, cache_b_v, cache_c_k, cache_c_v, c, c_ctx, w_ada, b_ada, w_in, a_q_norm, a_kv_norm, w_uq, w_ukv, b_q_norm, b_k_norm, c_rpb, w_bo_a, w_bo_b, w_bo_c, w_o, ln1_g, ln1_b, ln2_g, ln2_b, peer_wq, peer_k1, peer_k2, peer_u, peer_v):
    p = dict(w_in=w_in, a_q_norm=a_q_norm, a_kv_norm=a_kv_norm, w_uq=w_uq, w_ukv=w_ukv,
             b_q_norm=b_q_norm, b_k_norm=b_k_norm, w_bo_a=w_bo_a, w_bo_b=w_bo_b, w_bo_c=w_bo_c, w_o=w_o,
             ln1_g=ln1_g, ln1_b=ln1_b, ln2_g=ln2_g, ln2_b=ln2_b,
             peer_wq=peer_wq, peer_k1=peer_k1, peer_k2=peer_k2, peer_u=peer_u, peer_v=peer_v)
    Bp, S, D = x_prompt.shape
    Bs, T, _ = x_sample.shape
    depth = w_in.shape[0]
    ql, kl = a_q_norm.shape[1], a_kv_norm.shape[1]
    ha = w_uq.shape[2] // (NOPE_A + ROPE_A)
    hb = w_bo_b.shape[1] // HD_B
    gb = cache_b_k.shape[3]
    hc = cache_c_k.shape[3]
    dims = (D, ql, kl, ha, hb, gb, hc)
    alpha = (2 * depth) ** 0.25
    assert Bs + 1 <= SUBLANES

    cond = jnp.concatenate([c_ctx[None], c, jnp.zeros((SUBLANES - 1 - Bs, D), F32)], axis=0)
    mods = _modulations(cond, w_ada, b_ada)

    ctx = _Group(Bp, S, 0, Bp * S, rope=False)
    lat = _Group(Bs, T, 1, T, rope=True)
    b_cos, b_sin = _rope_tables(T, HD_B)
    a_cos, a_sin = _rope_tables(T, ROPE_A)
    tabs = dict(b_cos=b_cos, b_sin=b_sin, kr_cos=_pad_lanes(a_cos), kr_sin=_pad_lanes(a_sin),
                qa_cos=jnp.tile(_pad_lanes(a_cos), (1, ha)), qa_sin=jnp.tile(_pad_lanes(a_sin), (1, ha)))

    xp = x_prompt.reshape(Bp * S, D)
    xs = x_sample.reshape(Bs * T, D)
    new = [[] for _ in range(6)]
    for l in range(depth):
        lw = _layer_weights(l, p, dims)
        na_bias = _na_bias(c_rpb[l], T // GRID_W)
        xp, outs = _sublayers(xp, mods[l], lw, ctx, tabs, dims, None, None, alpha)
        for acc, o in zip(new, outs):
            acc.append(o)
        cache = (cache_a_ckv[:, l], cache_a_krope[:, l], cache_b_k[:, l], cache_b_v[:, l],
                 cache_c_k[:, l], cache_c_v[:, l])
        xs, _ = _sublayers(xs, mods[l], lw, lat, tabs, dims, cache, na_bias, alpha)

    def stack(parts, tail):
        return jnp.stack([a.reshape((Bp, S) + tail) for a in parts], axis=1)

    return (xp.reshape(Bp, S, D), xs.reshape(Bs, T, D),
            stack(new[0], (kl,)), stack(new[1], (ROPE_A,)),
            stack(new[2], (gb, HD_B)), stack(new[3], (gb, HD_B)),
            stack(new[4], (hc, HD_C)), stack(new[5], (hc, HD_C)))
```
